```python
import jax, jax.numpy as jnp
from jax import lax
import numpy as np

D_MODEL = 2048
BATCH = 4
SEQ = 8192
DEPTH = 1
DEC_BATCH = 8
DEC_SEQ = 32
PAST_LEN = 2048

CHUNK = 64
N_SUB = 3
FFN_RES = 0.5
D_FF = 5632
GMLP_CHUNK = 128
D_A = D_MODEL // 2
N_GROUPS_A = 8
GROUP_A = D_A // N_GROUPS_A
HEAD_DIM = 64
N_HEADS = (D_MODEL // 2) // HEAD_DIM
N_KV_HEADS = 4
Q_PER_KV = N_HEADS // N_KV_HEADS
D_B = N_HEADS * HEAD_DIM
D_KV = N_KV_HEADS * HEAD_DIM
WINDOW = 128
WINDOW_CHUNKS = WINDOW // CHUNK
BAND = (WINDOW_CHUNKS + 1) * CHUNK
ROT_DIM = HEAD_DIM // 4
ROPE_THETA = 500000.0
ATTN_SCALE = HEAD_DIM ** -0.5
NEG_INF = -1e30
EPS = 1e-6
SPLITS = (D_A, 2 * D_A, 2 * D_A + D_B, 2 * D_A + D_B + D_KV, 2 * D_A + D_B + 2 * D_KV,
          2 * D_A + D_B + 2 * D_KV + D_MODEL)
D_IN = 2 * D_A + D_B + 2 * D_KV + 2 * D_MODEL

kernel_name = 'chunk_stream_gmlp_swa_hybrid'


def _rmsnorm(x, g):
    xf = x.astype(jnp.float32)
    y = xf * lax.rsqrt(jnp.mean(xf * xf, axis=-1, keepdims=True) + EPS)
    return (y * g.astype(jnp.float32)).astype(x.dtype)


def _layernorm(x, g, b):
    xf = x.astype(jnp.float32)
    mu = jnp.mean(xf, axis=-1, keepdims=True)
    var = jnp.mean(jnp.square(xf - mu), axis=-1, keepdims=True)
    y = (xf - mu) * lax.rsqrt(var + EPS)
    return (y * g.astype(jnp.float32) + b.astype(jnp.float32)).astype(x.dtype)


def _adaln(c, w_ada, b_ada):
    mod = jax.nn.silu(c) @ w_ada + b_ada
    return mod.reshape(c.shape[0], N_SUB, 3, D_MODEL)


def _modulate(h, shift, scale):
    return h * (1 + scale[:, None, :]) + shift[:, None, :]


def _ffn_half(x, mod, g, w1, w3, w2):
    h = _modulate(_rmsnorm(x, g), mod[:, 0], mod[:, 1])
    y = (jax.nn.silu(h @ w1) * (h @ w3)) @ w2
    return x + FFN_RES * mod[:, 2][:, None, :] * y


def _rope(x, pos):
    inv_freq = ROPE_THETA ** (-jnp.arange(0, ROT_DIM, 2, dtype=jnp.float32) / ROT_DIM)
    ang = pos.astype(jnp.float32)[:, None] * inv_freq[None, :]
    cos = jnp.cos(ang)[:, None, :]
    sin = jnp.sin(ang)[:, None, :]
    xr = x[..., :ROT_DIM].astype(jnp.float32)
    x1, x2 = xr[..., :ROT_DIM // 2], xr[..., ROT_DIM // 2:]
    rot = jnp.concatenate([x1 * cos - x2 * sin, x2 * cos + x1 * sin], axis=-1).astype(x.dtype)
    return jnp.concatenate([rot, x[..., ROT_DIM:]], axis=-1)


def _mixer_inputs(x, mod, lw, pos):
    b, s, _ = x.shape
    h = _modulate(_rmsnorm(x, lw['g_mix']), mod[:, 0], mod[:, 1])
    proj = h @ lw['w_in']
    u, v, q, k, va, ga, gb = jnp.split(proj, SPLITS, axis=-1)
    u = jax.nn.gelu(u, approximate=False)
    v_n = _layernorm(jax.nn.gelu(v, approximate=False), lw['ln_v_g'], lw['ln_v_b'])
    q = _rope(_rmsnorm(q.reshape(b, s, N_HEADS, HEAD_DIM), lw['g_q']), pos)
    k = _rope(_rmsnorm(k.reshape(b, s, N_KV_HEADS, HEAD_DIM), lw['g_k']), pos)
    va = va.reshape(b, s, N_KV_HEADS, HEAD_DIM)
    return u, v_n, q, k, va, ga, gb


def _gmlp_mask(dtype):
    i = np.arange(GMLP_CHUNK)
    return jnp.asarray((i[None, :] // CHUNK) <= (i[:, None] // CHUNK), dtype=dtype)


def _spatial_gate(u, v_n, w_s, b_s):
    b, s, _ = u.shape
    l = min(s, GMLP_CHUNK)
    w = (w_s * _gmlp_mask(w_s.dtype))[:, :l, :l]
    vb = v_n.reshape(b, s // l, l, N_GROUPS_A, GROUP_A)
    sv = jnp.einsum('gij,bnjgc->bnigc', w, vb) + b_s[:, :l].T[None, None, :, :, None]
    return u * sv.reshape(b, s, D_A)


def _sink_attend(qb, kb, vb, sinks, valid):
    s = jnp.einsum('bnqhgd,bnkhd->bnhgqk', qb, kb).astype(jnp.float32) * ATTN_SCALE
    s = jnp.where(valid[None, :, None, None, None, :], s, NEG_INF)
    sink = sinks.astype(jnp.float32).reshape(N_KV_HEADS, Q_PER_KV)[None, None, :, :, None, None]
    sink = jnp.broadcast_to(sink, s.shape[:-1] + (1,))
    p = jax.nn.softmax(jnp.concatenate([s, sink], axis=-1), axis=-1)[..., :-1]
    return jnp.einsum('bnhgqk,bnkhd->bnqhgd', p.astype(vb.dtype), vb)


def _swa_prompt(q, k, v, sinks):
    b, s = q.shape[:2]
    nc = s // CHUNK
    pad = WINDOW_CHUNKS * CHUNK

    def band(t):
        tp = jnp.pad(t, ((0, 0), (pad, 0), (0, 0), (0, 0)))
        tp = tp.reshape(b, nc + WINDOW_CHUNKS, CHUNK, N_KV_HEADS, HEAD_DIM)
        return jnp.concatenate([tp[:, i:i + nc] for i in range(WINDOW_CHUNKS + 1)], axis=2)

    key_pos = jnp.arange(nc)[:, None] * CHUNK + jnp.arange(BAND)[None, :] - pad
    qb = q.reshape(b, nc, CHUNK, N_KV_HEADS, Q_PER_KV, HEAD_DIM)
    o = _sink_attend(qb, band(k), band(v), sinks, key_pos >= 0)
    return o.reshape(b, s, D_B)


def _swa_sample(q, k, v, cache_k, cache_v, sinks):
    b, n = q.shape[:2]
    k_all = jnp.concatenate([cache_k.astype(k.dtype), k], axis=1)
    v_all = jnp.concatenate([cache_v.astype(v.dtype), v], axis=1)
    valid = jnp.ones((1, k_all.shape[1]), dtype=bool)
    qb = q.reshape(b, 1, n, N_KV_HEADS, Q_PER_KV, HEAD_DIM)
    o = _sink_attend(qb, k_all[:, None], v_all[:, None], sinks, valid)
    win = cache_k.shape[1]
    return o.reshape(b, n, D_B), k_all[:, -win:], v_all[:, -win:]


def _merge(x, mod, a, o, ga, gb, lw):
    y = jax.nn.sigmoid(ga) * (a @ lw['w_pa']) + jax.nn.sigmoid(gb) * (o @ lw['w_pb'])
    return x + mod[:, 2][:, None, :] * (y @ lw['w_o'])


def _layer_prompt(x, c, lw):
    mod = _adaln(c, lw['w_ada'], lw['b_ada'])
    x = _ffn_half(x, mod[:, 0], lw['g_ffn1'], lw['w1_ffn1'], lw['w3_ffn1'], lw['w2_ffn1'])
    pos = jnp.arange(x.shape[1])
    u, v_n, q, k, v, ga, gb = _mixer_inputs(x, mod[:, 1], lw, pos)
    a = _spatial_gate(u, v_n, lw['w_s'], lw['b_s'])
    o = _swa_prompt(q, k, v, lw['sinks'])
    x = _merge(x, mod[:, 1], a, o, ga, gb, lw)
    x = _ffn_half(x, mod[:, 2], lw['g_ffn2'], lw['w1_ffn2'], lw['w3_ffn2'], lw['w2_ffn2'])
    return x, k, v


def _layer_sample(x, c, cache_k, cache_v, lw):
    mod = _adaln(c, lw['w_ada'], lw['b_ada'])
    x = _ffn_half(x, mod[:, 0], lw['g_ffn1'], lw['w1_ffn1'], lw['w3_ffn1'], lw['w2_ffn1'])
    pos = PAST_LEN + jnp.arange(x.shape[1])
    u, v_n, q, k, v, ga, gb = _mixer_inputs(x, mod[:, 1], lw, pos)
    a = _spatial_gate(u, v_n, lw['w_s'], lw['b_s'])
    o, k_win, v_win = _swa_sample(q, k, v, cache_k, cache_v, lw['sinks'])
    x = _merge(x, mod[:, 1], a, o, ga, gb, lw)
    x = _ffn_half(x, mod[:, 2], lw['g_ffn2'], lw['w1_ffn2'], lw['w3_ffn2'], lw['w2_ffn2'])
    return x, k_win, v_win, v_n


def setup_inputs(seed: int = 0) -> dict:
    key = jax.random.key(seed)
    ks = jax.random.split(key, 28)

    def nrm(k, shape, scale):
        return jax.random.normal(k, shape, jnp.float32) * scale

    win = min(WINDOW, PAST_LEN)
    L = DEPTH
    return {
        'x_prompt': nrm(ks[0], (BATCH, SEQ, D_MODEL), 1.0),
        'x_sample': nrm(ks[1], (DEC_BATCH, DEC_SEQ, D_MODEL), 1.0),
        'cache_swa_k': nrm(ks[2], (L, DEC_BATCH, win, N_KV_HEADS, HEAD_DIM), 1.0),
        'cache_swa_v': nrm(ks[3], (L, DEC_BATCH, win, N_KV_HEADS, HEAD_DIM), 1.0),
        'c_prompt': nrm(ks[4], (BATCH, D_MODEL), 1.0),
        'c_sample': nrm(ks[5], (DEC_BATCH, D_MODEL), 1.0),
        'w_ada': nrm(ks[6], (L, D_MODEL, N_SUB * 3 * D_MODEL), 0.5 * D_MODEL ** -0.5),
        'b_ada': nrm(ks[7], (L, N_SUB * 3 * D_MODEL), 0.02),
        'g_ffn1': 1.0 + nrm(ks[8], (L, D_MODEL), 0.05),
        'w1_ffn1': nrm(ks[9], (L, D_MODEL, D_FF), D_MODEL ** -0.5),
        'w3_ffn1': nrm(ks[10], (L, D_MODEL, D_FF), D_MODEL ** -0.5),
        'w2_ffn1': nrm(ks[11], (L, D_FF, D_MODEL), D_FF ** -0.5),
        'g_mix': 1.0 + nrm(ks[12], (L, D_MODEL), 0.05),
        'w_in': nrm(ks[13], (L, D_MODEL, D_IN), D_MODEL ** -0.5),
        'g_q': 1.0 + nrm(ks[14], (L, HEAD_DIM), 0.05),
        'g_k': 1.0 + nrm(ks[15], (L, HEAD_DIM), 0.05),
        'ln_v_g': 1.0 + nrm(ks[16], (L, D_A), 0.05),
        'ln_v_b': nrm(ks[17], (L, D_A), 0.02),
        'w_s': nrm(ks[18], (L, N_GROUPS_A, GMLP_CHUNK, GMLP_CHUNK), GMLP_CHUNK ** -0.5),
        'b_s': 1.0 + nrm(ks[19], (L, N_GROUPS_A, GMLP_CHUNK), 0.05),
        'sinks': nrm(ks[20], (L, N_HEADS), 0.5),
        'w_pa': nrm(ks[21], (L, D_A, D_MODEL), D_A ** -0.5),
        'w_pb': nrm(ks[22], (L, D_B, D_MODEL), D_B ** -0.5),
        'w_o': nrm(ks[23], (L, D_MODEL, D_MODEL), D_MODEL ** -0.5),
        'g_ffn2': 1.0 + nrm(ks[24], (L, D_MODEL), 0.05),
        'w1_ffn2': nrm(ks[25], (L, D_MODEL, D_FF), D_MODEL ** -0.5),
        'w3_ffn2': nrm(ks[26], (L, D_MODEL, D_FF), D_MODEL ** -0.5),
        'w2_ffn2': nrm(ks[27], (L, D_FF, D_MODEL), D_FF ** -0.5),
    }


def reference(x_prompt, x_sample, cache_swa_k, cache_swa_v, c_prompt, c_sample, w_ada, b_ada,
              g_ffn1, w1_ffn1, w3_ffn1, w2_ffn1, g_mix, w_in, g_q, g_k, ln_v_g, ln_v_b, w_s, b_s,
              sinks, w_pa, w_pb, w_o, g_ffn2, w1_ffn2, w3_ffn2, w2_ffn2):
    win = cache_swa_k.shape[2]
    y_p, y_s = x_prompt, x_sample
    kp_l, vp_l, ks_l, vs_l, gv_l = [], [], [], [], []
    for l in range(DEPTH):
        lw = {'w_ada': w_ada[l], 'b_ada': b_ada[l],
              'g_ffn1': g_ffn1[l], 'w1_ffn1': w1_ffn1[l], 'w3_ffn1': w3_ffn1[l], 'w2_ffn1': w2_ffn1[l],
              'g_mix': g_mix[l], 'w_in': w_in[l], 'g_q': g_q[l], 'g_k': g_k[l],
              'ln_v_g': ln_v_g[l], 'ln_v_b': ln_v_b[l], 'w_s': w_s[l], 'b_s': b_s[l],
              'sinks': sinks[l], 'w_pa': w_pa[l], 'w_pb': w_pb[l], 'w_o': w_o[l],
              'g_ffn2': g_ffn2[l], 'w1_ffn2': w1_ffn2[l], 'w3_ffn2': w3_ffn2[l], 'w2_ffn2': w2_ffn2[l]}
        y_p, k_p, v_p = _layer_prompt(y_p, c_prompt, lw)
        y_s, k_s, v_s, vn_s = _layer_sample(y_s, c_sample, cache_swa_k[l], cache_swa_v[l], lw)
        kp_l.append(k_p[:, -win:])
        vp_l.append(v_p[:, -win:])
        ks_l.append(k_s)
        vs_l.append(v_s)
        gv_l.append(vn_s)
    swa_k_prompt = jnp.stack(kp_l)
    swa_v_prompt = jnp.stack(vp_l)
    swa_k_sample = jnp.stack(ks_l)
    swa_v_sample = jnp.stack(vs_l)
    gmlp_v_sample = jnp.stack(gv_l)
    return (y_p, y_s, swa_k_prompt, swa_v_prompt, swa_k_sample, swa_v_sample, gmlp_v_sample)
```

```python
import functools

import jax
import jax.numpy as jnp
import numpy as np
from jax import lax
from jax.experimental import pallas as pl
from jax.experimental.pallas import tpu as pltpu

F32 = jnp.float32
BF16 = jnp.bfloat16

D_MODEL = 2048
N_SUB = 3
FFN_RES = 0.5
D_FF = 5632
CHUNK = 64
GMLP_CHUNK = 128
D_A = D_MODEL // 2
N_GROUPS_A = 8
GROUP_A = D_A // N_GROUPS_A
HEAD_DIM = 64
N_HEADS = 16
N_KV_HEADS = 4
Q_PER_KV = N_HEADS // N_KV_HEADS
D_B = N_HEADS * HEAD_DIM
D_KV = N_KV_HEADS * HEAD_DIM
WINDOW = 128
ROT_DIM = HEAD_DIM // 4
ROPE_THETA = 500000.0
ATTN_SCALE = HEAD_DIM ** -0.5
NEG_INF = -1e30
EPS = 1e-6
PAST_LEN = 2048
D_IN = 2 * D_A + D_B + 2 * D_KV + 2 * D_MODEL
GATE_A_COL = 2 * D_A + D_B + 2 * D_KV
GATE_B_COL = GATE_A_COL + D_MODEL

LANE = 128
KEY_TILE = 2 * WINDOW
V7X_VMEM_LIMIT = 56 * 1024 * 1024


def _dot(a, b):
    return jnp.dot(a, b, preferred_element_type=F32)


def _params(sem, vmem=V7X_VMEM_LIMIT):
    return pltpu.CompilerParams(dimension_semantics=sem, vmem_limit_bytes=vmem)


def _row_spec(bb, ts, w):
    return pl.BlockSpec((bb, ts, w), lambda b, s, *_: (b, s, 0))


def _mod_spec(bb):
    return pl.BlockSpec((bb, 1, D_MODEL), lambda b, s, *_: (b, 0, 0))


def _const_spec(shape):
    nd = len(shape)
    return pl.BlockSpec(shape, lambda *_: (0,) * nd)


def _norm_modulate(x, g, shift, scale):
    bb, ts, d = x.shape
    y = x * lax.rsqrt(jnp.mean(x * x, axis=-1, keepdims=True) + EPS)
    h = (y * g) * (1.0 + scale) + shift
    return h.reshape(bb * ts, d).astype(BF16)


def _adaln_kernel(c_ref, w_ref, b_ref, o_ref):
    c = c_ref[...]
    o_ref[...] = _dot(jax.nn.silu(c).astype(BF16), w_ref[...].astype(BF16)) + b_ref[...]


def _adaln(c_all, w_ada, b_ada):
    rows = c_all.shape[0]
    n_out = w_ada.shape[1]
    tn = 1024
    return pl.pallas_call(
        _adaln_kernel,
        grid=(n_out // tn,),
        in_specs=[pl.BlockSpec((rows, D_MODEL), lambda n: (0, 0)),
                  pl.BlockSpec((D_MODEL, tn), lambda n: (0, n)),
                  pl.BlockSpec((1, tn), lambda n: (0, n))],
        out_specs=pl.BlockSpec((rows, tn), lambda n: (0, n)),
        out_shape=jax.ShapeDtypeStruct((rows, n_out), F32),
        compiler_params=_params(("arbitrary",)),
        name="adaln",
    )(c_all, w_ada, b_ada.reshape(1, n_out))


def _ffn_kernel(x_ref, sh_ref, sc_ref, gt_ref, g_ref, w1_ref, w3_ref, w2_ref, o_ref, h_scr):
    f = pl.program_id(2)
    bb, ts, d = x_ref.shape

    @pl.when(f == 0)
    def _():
        h_scr[...] = _norm_modulate(x_ref[...], g_ref[...], sh_ref[...], sc_ref[...])

    h = h_scr[...]
    act = jax.nn.silu(_dot(h, w1_ref[...])) * _dot(h, w3_ref[...])
    y = _dot(act.astype(BF16), w2_ref[...]).reshape(bb, ts, d)

    @pl.when(f == 0)
    def _():
        o_ref[...] = y

    @pl.when(f > 0)
    def _():
        o_ref[...] += y

    @pl.when(f == pl.num_programs(2) - 1)
    def _():
        o_ref[...] = x_ref[...] + (FFN_RES * gt_ref[...]) * o_ref[...]


def _ffn_half(x, mod, g, w1, w3, w2, *, bb, ts, tf):
    nb, s, d = x.shape
    grid = (nb // bb, s // ts, D_FF // tf)
    return pl.pallas_call(
        _ffn_kernel,
        grid=grid,
        in_specs=[_row_spec(bb, ts, d), _mod_spec(bb), _mod_spec(bb), _mod_spec(bb),
                  _const_spec((1, 1, d)),
                  pl.BlockSpec((d, tf), lambda b, s, f: (0, f)),
                  pl.BlockSpec((d, tf), lambda b, s, f: (0, f)),
                  pl.BlockSpec((tf, d), lambda b, s, f: (f, 0))],
        out_specs=_row_spec(bb, ts, d),
        out_shape=jax.ShapeDtypeStruct(x.shape, F32),
        scratch_shapes=[pltpu.VMEM((bb * ts, d), BF16)],
        compiler_params=_params(("parallel", "parallel", "arbitrary")),
        name="ffn_half",
    )(x, mod[0], mod[1], mod[2], g.reshape(1, 1, d), w1, w3, w2)


def _gelu(x):
    return 0.5 * x * (1.0 + lax.erf(x * np.float32(np.sqrt(0.5))))


def _split_bf16(x):
    hi = x.astype(BF16)
    lo = (x - hi.astype(F32)).astype(BF16)
    return hi, lo


def _head_rmsnorm(acc, seg, seg_t, gain):
    n = acc.shape[1]
    hi, lo = _split_bf16(acc * acc)
    ss = _dot(hi, seg[:n]) + _dot(lo, seg[:n])
    r = lax.rsqrt(ss * (1.0 / HEAD_DIM) + EPS)
    rhi, rlo = _split_bf16(r)
    rb = _dot(rhi, seg_t[:, :n]) + _dot(rlo, seg_t[:, :n])
    return acc * rb * gain


def _rope_tile(xt, rope_ref):
    return (xt * rope_ref[0] + pltpu.roll(xt, LANE - ROT_DIM // 2, 1) * rope_ref[1]
            + pltpu.roll(xt, ROT_DIM // 2, 1) * rope_ref[2])


def _mixer_in_kernel(x_ref, sh_ref, sc_ref, g_ref, wm_ref, wkv_ref, lng_ref, lnb_ref, gq_ref, gk_ref,
                     seg_ref, segt_ref, rope_ref, *out_refs_and_scratch, emit_vn_f32):
    if emit_vn_f32:
        u_ref, vn_ref, q_ref, k_ref, va_ref, vnf_ref, h_scr = out_refs_and_scratch
    else:
        u_ref, vn_ref, q_ref, k_ref, va_ref, h_scr = out_refs_and_scratch
        vnf_ref = None
    j = pl.program_id(2)
    bb, ts, _ = x_ref.shape

    @pl.when(j == 0)
    def _():
        h_scr[...] = _norm_modulate(x_ref[...], g_ref[...], sh_ref[...], sc_ref[...])
        u = _gelu(_dot(h_scr[...], wm_ref[...]))
        u_ref[...] = u.reshape(bb, ts, D_A).astype(BF16)

    @pl.when(j == 1)
    def _():
        gv = _gelu(_dot(h_scr[...], wm_ref[...]))
        mu = jnp.mean(gv, axis=-1, keepdims=True)
        dv = gv - mu
        var = jnp.mean(dv * dv, axis=-1, keepdims=True)
        vn = (dv * lax.rsqrt(var + EPS)) * lng_ref[...] + lnb_ref[...]
        vn_ref[...] = vn.reshape(bb, ts, D_A).astype(BF16)
        if vnf_ref is not None:
            vnf_ref[...] = vn.reshape(bb, ts, D_A)

    @pl.when(j == 2)
    def _():
        qn = _head_rmsnorm(_dot(h_scr[...], wm_ref[...]), seg_ref[...], segt_ref[...], gq_ref[...])
        for t in range(D_B // LANE):
            piece = _rope_tile(qn[:, t * LANE:(t + 1) * LANE], rope_ref) * ATTN_SCALE
            q_ref[:, :, t * LANE:(t + 1) * LANE] = piece.reshape(bb, ts, LANE).astype(BF16)

    @pl.when(j == 3)
    def _():
        kv = _dot(h_scr[...], wkv_ref[...])
        kn = _head_rmsnorm(kv[:, :D_KV], seg_ref[...], segt_ref[...], gk_ref[...])
        for t in range(D_KV // LANE):
            piece = _rope_tile(kn[:, t * LANE:(t + 1) * LANE], rope_ref)
            k_ref[:, :, t * LANE:(t + 1) * LANE] = piece.reshape(bb, ts, LANE)
        va_ref[...] = kv[:, D_KV:].reshape(bb, ts, D_KV)


def _mixer_in(x, mod, lw, consts, rope, *, bb, ts, emit_vn_f32):
    nb, s, d = x.shape
    m = bb * ts
    grid = (nb // bb, s // ts, 4)
    out_shape = [jax.ShapeDtypeStruct((nb, s, D_A), BF16),
                 jax.ShapeDtypeStruct((nb, s, D_A), BF16),
                 jax.ShapeDtypeStruct((nb, s, D_B), BF16),
                 jax.ShapeDtypeStruct((nb, s, D_KV), F32),
                 jax.ShapeDtypeStruct((nb, s, D_KV), F32)]
    out_specs = [_row_spec(bb, ts, D_A), _row_spec(bb, ts, D_A), _row_spec(bb, ts, D_B),
                 _row_spec(bb, ts, D_KV), _row_spec(bb, ts, D_KV)]
    if emit_vn_f32:
        out_shape.append(jax.ShapeDtypeStruct((nb, s, D_A), F32))
        out_specs.append(_row_spec(bb, ts, D_A))
    main_cols = 2 * D_A + D_B
    return pl.pallas_call(
        functools.partial(_mixer_in_kernel, emit_vn_f32=emit_vn_f32),
        grid=grid,
        in_specs=[_row_spec(bb, ts, d), _mod_spec(bb), _mod_spec(bb), _const_spec((1, 1, d)),
                  pl.BlockSpec((d, D_A), lambda b, s, j: (0, jnp.minimum(j, main_cols // D_A - 1))),
                  pl.BlockSpec((d, 2 * D_KV), lambda b, s, j: (0, main_cols // (2 * D_KV))),
                  _const_spec((1, D_A)), _const_spec((1, D_A)),
                  _const_spec((1, D_B)), _const_spec((1, D_KV)),
                  _const_spec((D_B, LANE)), _const_spec((LANE, D_B)),
                  pl.BlockSpec((3, m, LANE), lambda b, s, j: (0, s, 0))],
        out_specs=out_specs,
        out_shape=out_shape,
        scratch_shapes=[pltpu.VMEM((m, d), BF16)],
        compiler_params=_params(("parallel", "parallel", "arbitrary")),
        name="mixer_in",
    )(x, mod[0], mod[1], lw["g_mix"].reshape(1, 1, d), lw["w_in"], lw["w_in"],
      lw["ln_v_g"].reshape(1, D_A), lw["ln_v_b"].reshape(1, D_A),
      jnp.tile(lw["g_q"], N_HEADS).reshape(1, D_B), jnp.tile(lw["g_k"], N_KV_HEADS).reshape(1, D_KV),
      consts["seg"], consts["seg_t"], rope)


def _expand_heads(t):
    half = HEAD_DIM
    lane_half = lax.broadcasted_iota(jnp.int32, (t.shape[0], LANE), 1) // half
    zeros = jnp.zeros((t.shape[0], LANE), t.dtype)
    out = []
    for g in range(N_KV_HEADS):
        src = t[:, (g // 2) * LANE:(g // 2 + 1) * LANE]
        swapped = pltpu.roll(src, half, 1)
        blocks = []
        for j in range(Q_PER_KV):
            piece = src if (g % 2) == (j % 2) else swapped
            piece = jnp.where(lane_half == (j % 2), piece, zeros)
            row = [piece, zeros] if j // 2 == 0 else [zeros, piece]
            blocks.append(jnp.concatenate(row, axis=1))
        out.append(jnp.concatenate(blocks, axis=0))
    return out


def _attend_block(q, kt, vt, sinks_ref, valid):
    k_exp = _expand_heads(kt)
    v_exp = _expand_heads(vt)
    width = Q_PER_KV * HEAD_DIM
    outs = []
    for g in range(N_KV_HEADS):
        qg = q[:, g * width:(g + 1) * width]
        s = lax.dot_general(qg, k_exp[g], (((1,), (1,)), ((), ())), preferred_element_type=F32)
        probs = []
        for j in range(Q_PER_KV):
            sj = jnp.where(valid, s[:, j * KEY_TILE:(j + 1) * KEY_TILE], NEG_INF)
            sink = sinks_ref[g * Q_PER_KV + j]
            m = jnp.maximum(jnp.max(sj, axis=-1, keepdims=True), sink)
            p = jnp.exp(sj - m)
            den = jnp.sum(p, axis=-1, keepdims=True) + jnp.exp(sink - m)
            probs.append((p / den).astype(BF16))
        outs.append(_dot(jnp.concatenate(probs, axis=1), v_exp[g]))
    return jnp.concatenate(outs, axis=1)


def _attn_prompt_kernel(sinks_ref, q_ref, kp_ref, kc_ref, vp_ref, vc_ref, o_ref):
    i = pl.program_id(1)
    tq = q_ref.shape[1]
    kk = jnp.concatenate([kp_ref[0], kc_ref[0]], axis=0).astype(BF16)
    vv = jnp.concatenate([vp_ref[0], vc_ref[0]], axis=0).astype(BF16)
    row_chunk = lax.broadcasted_iota(jnp.int32, (WINDOW, KEY_TILE), 0) // CHUNK
    col = lax.broadcasted_iota(jnp.int32, (WINDOW, KEY_TILE), 1)
    col_chunk = col // CHUNK
    band = (col_chunk >= row_chunk) & (col_chunk <= row_chunk + WINDOW // CHUNK)
    for sub in range(tq // WINDOW):
        r0 = sub * WINDOW
        valid = band & (col + (i * tq + r0 - WINDOW) >= 0)
        o = _attend_block(q_ref[0, r0:r0 + WINDOW, :], kk[r0:r0 + KEY_TILE], vv[r0:r0 + KEY_TILE],
                          sinks_ref, valid)
        o_ref[0, r0:r0 + WINDOW, :] = o.astype(BF16)


def _attn_prompt(q, k, v, sinks, *, tq):
    nb, s, _ = q.shape
    per = tq // WINDOW

    def prev_map(b, i):
        return (b, jnp.maximum(i * per - 1, 0), 0)

    cur = lambda w: pl.BlockSpec((1, tq, w), lambda b, i: (b, i, 0))
    prev = pl.BlockSpec((1, WINDOW, D_KV), prev_map)
    return pl.pallas_call(
        _attn_prompt_kernel,
        grid=(nb, s // tq),
        in_specs=[pl.BlockSpec(memory_space=pltpu.SMEM), cur(D_B), prev, cur(D_KV), prev, cur(D_KV)],
        out_specs=cur(D_B),
        out_shape=jax.ShapeDtypeStruct((nb, s, D_B), BF16),
        compiler_params=_params(("parallel", "arbitrary")),
        name="attn_prompt",
    )(sinks, q, k, k, v, v)


def _attn_sample_kernel(sinks_ref, q_ref, kn_ref, vn_ref, ck_ref, cv_ref, o_ref, kw_ref, vw_ref, k_scr, v_scr):
    n = q_ref.shape[1]
    win = ck_ref.shape[1]
    for src_c, src_n, scr, w_ref in ((ck_ref, kn_ref, k_scr, kw_ref), (cv_ref, vn_ref, v_scr, vw_ref)):
        scr[0:win, :] = src_c[0]
        scr[win:win + n, :] = src_n[0]
        scr[win + n:, :] = jnp.zeros((KEY_TILE - win - n, D_KV), F32)
        w_ref[0] = scr[n:win + n, :]
    valid = lax.broadcasted_iota(jnp.int32, (n, KEY_TILE), 1) < win + n
    o = _attend_block(q_ref[0], k_scr[...].astype(BF16), v_scr[...].astype(BF16), sinks_ref, valid)
    o_ref[0] = o.astype(BF16)


def _attn_sample(q, k, v, cache_k, cache_v, sinks):
    nb, n, _ = q.shape
    win = cache_k.shape[1]
    blk = lambda r, w: pl.BlockSpec((1, r, w), lambda b: (b, 0, 0))
    return pl.pallas_call(
        _attn_sample_kernel,
        grid=(nb,),
        in_specs=[pl.BlockSpec(memory_space=pltpu.SMEM), blk(n, D_B), blk(n, D_KV), blk(n, D_KV),
                  blk(win, D_KV), blk(win, D_KV)],
        out_specs=[blk(n, D_B), blk(win, D_KV), blk(win, D_KV)],
        out_shape=[jax.ShapeDtypeStruct((nb, n, D_B), BF16),
                   jax.ShapeDtypeStruct((nb, win, D_KV), F32),
                   jax.ShapeDtypeStruct((nb, win, D_KV), F32)],
        scratch_shapes=[pltpu.VMEM((KEY_TILE, D_KV), F32), pltpu.VMEM((KEY_TILE, D_KV), F32)],
        compiler_params=_params(("arbitrary",)),
        name="attn_sample",
    )(sinks, q, k, v, cache_k, cache_v)


def _mixer_out_kernel(x_ref, sh_ref, sc_ref, gt_ref, g_ref, u_ref, vn_ref, o_in_ref, ws_ref, bs_ref,
                      wpa_ref, wpb_ref, wga_ref, wgb_ref, wo_ref, o_ref, h_scr, a_scr, *, chunk):
    n = pl.program_id(2)
    bb, ts, d = x_ref.shape
    m = bb * ts

    @pl.when(n == 0)
    def _():
        h_scr[...] = _norm_modulate(x_ref[...], g_ref[...], sh_ref[...], sc_ref[...])
        row_c = lax.broadcasted_iota(jnp.int32, (chunk, chunk), 0) // CHUNK
        col_c = lax.broadcasted_iota(jnp.int32, (chunk, chunk), 1) // CHUNK
        mask = (col_c <= row_c).astype(F32)
        for g in range(N_GROUPS_A):
            w = (ws_ref[g, :chunk, :chunk] * mask).astype(BF16)
            bias = bs_ref[:, g:g + 1]
            cols = slice(g * GROUP_A, (g + 1) * GROUP_A)
            for c in range(m // chunk):
                b_idx, s_off = divmod(c * chunk, ts)
                rows = slice(s_off, s_off + chunk)
                sv = _dot(w, vn_ref[b_idx, rows, cols]) + bias
                a = u_ref[b_idx, rows, cols].astype(F32) * sv
                a_scr[c * chunk:(c + 1) * chunk, cols] = a.astype(BF16)

    h = h_scr[...]
    ya = _dot(a_scr[...], wpa_ref[...])
    yb = _dot(o_in_ref[...].reshape(m, D_B), wpb_ref[...])
    y = jax.nn.sigmoid(_dot(h, wga_ref[...])) * ya + jax.nn.sigmoid(_dot(h, wgb_ref[...])) * yb
    z = _dot(y.astype(BF16), wo_ref[...]).reshape(bb, ts, d)

    @pl.when(n == 0)
    def _():
        o_ref[...] = z

    @pl.when(n > 0)
    def _():
        o_ref[...] += z

    @pl.when(n == pl.num_programs(2) - 1)
    def _():
        o_ref[...] = x_ref[...] + gt_ref[...] * o_ref[...]


def _mixer_out(x, mod, lw, u, vn, o, *, bb, ts, tn, chunk):
    nb, s, d = x.shape
    m = bb * ts
    grid = (nb // bb, s // ts, d // tn)
    ga0, gb0 = GATE_A_COL // tn, GATE_B_COL // tn
    return pl.pallas_call(
        functools.partial(_mixer_out_kernel, chunk=chunk),
        grid=grid,
        in_specs=[_row_spec(bb, ts, d), _mod_spec(bb), _mod_spec(bb), _mod_spec(bb), _const_spec((1, 1, d)),
                  _row_spec(bb, ts, D_A), _row_spec(bb, ts, D_A), _row_spec(bb, ts, D_B),
                  _const_spec((N_GROUPS_A, GMLP_CHUNK, GMLP_CHUNK)), _const_spec((chunk, N_GROUPS_A)),
                  pl.BlockSpec((D_A, tn), lambda b, s, n: (0, n)),
                  pl.BlockSpec((D_B, tn), lambda b, s, n: (0, n)),
                  pl.BlockSpec((d, tn), lambda b, s, n: (0, ga0 + n)),
                  pl.BlockSpec((d, tn), lambda b, s, n: (0, gb0 + n)),
                  pl.BlockSpec((tn, d), lambda b, s, n: (n, 0))],
        out_specs=_row_spec(bb, ts, d),
        out_shape=jax.ShapeDtypeStruct(x.shape, F32),
        scratch_shapes=[pltpu.VMEM((m, d), BF16), pltpu.VMEM((m, D_A), BF16)],
        compiler_params=_params(("parallel", "parallel", "arbitrary")),
        name="mixer_out",
    )(x, mod[0], mod[1], mod[2], lw["g_mix"].reshape(1, 1, d), u, vn, o,
      lw["w_s"], lw["b_s"][:, :chunk].T, lw["w_pa"], lw["w_pb"], lw["w_in"], lw["w_in"], lw["w_o"])


def _rope_tables(pos, reps):
    inv_freq = ROPE_THETA ** (-jnp.arange(0, ROT_DIM, 2, dtype=F32) / ROT_DIM)
    ang = pos.astype(F32)[:, None] * inv_freq[None, :]
    cos, sin = jnp.cos(ang), jnp.sin(ang)
    half = ROT_DIM // 2
    ones = jnp.ones((pos.shape[0], HEAD_DIM - ROT_DIM), F32)
    zeros = jnp.zeros((pos.shape[0], HEAD_DIM - ROT_DIM), F32)
    zh = jnp.zeros((pos.shape[0], half), F32)
    c_tab = jnp.concatenate([cos, cos, ones], axis=1)
    s_up = jnp.concatenate([-sin, zh, zeros], axis=1)
    s_dn = jnp.concatenate([zh, sin, zeros], axis=1)
    tab = jnp.stack([c_tab, s_up, s_dn])
    return jnp.tile(tab, (1, reps, LANE // HEAD_DIM))


def _segment_consts():
    seg = (np.arange(D_B)[:, None] // HEAD_DIM == np.arange(LANE)[None, :]).astype(np.float32)
    return {"seg": jnp.asarray(seg, BF16), "seg_t": jnp.asarray(seg.T, BF16)}


def _layer(x, mods, lw, consts, rope, attend, *, bb, ts, tf, tn, chunk, emit_vn_f32):
    x = _ffn_half(x, mods[0], lw["g_ffn1"], lw["w1_ffn1"], lw["w3_ffn1"], lw["w2_ffn1"], bb=bb, ts=ts, tf=tf)
    outs = _mixer_in(x, mods[1], lw, consts, rope, bb=bb, ts=ts, emit_vn_f32=emit_vn_f32)
    u, vn, q, k, va = outs[:5]
    o, extras = attend(q, k, va)
    x = _mixer_out(x, mods[1], lw, u, vn, o, bb=bb, ts=ts, tn=tn, chunk=chunk)
    x = _ffn_half(x, mods[2], lw["g_ffn2"], lw["w1_ffn2"], lw["w3_ffn2"], lw["w2_ffn2"], bb=bb, ts=ts, tf=tf)
    return x, k, va, extras, outs[5:]


def kernel(x_prompt, x_sample, cache_swa_k, cache_swa_v, c_prompt, c_sample, w_ada, b_ada,
           g_ffn1, w1_ffn1, w3_ffn1, w2_ffn1, g_mix, w_in, g_q, g_k, ln_v_g, ln_v_b, w_s, b_s,
           sinks, w_pa, w_pb, w_o, g_ffn2, w1_ffn2, w3_ffn2, w2_ffn2):
    depth = w_ada.shape[0]
    nb_p, seq, _ = x_prompt.shape
    nb_s, dec_seq, _ = x_sample.shape
    win = cache_swa_k.shape[2]
    consts = _segment_consts()
    rope_p = _rope_tables(jnp.arange(seq), 1)
    rope_s = _rope_tables(PAST_LEN + jnp.arange(dec_seq), nb_s)
    pad = (-(nb_p + nb_s)) % 8
    c_all = jnp.concatenate([c_prompt, c_sample, jnp.zeros((pad, D_MODEL), F32)], axis=0)

    y_p, y_s = x_prompt, x_sample
    kp_l, vp_l, ks_l, vs_l, gv_l = [], [], [], [], []
    for l in range(depth):
        lw = {"g_ffn1": g_ffn1[l], "g_mix": g_mix[l], "g_q": g_q[l], "g_k": g_k[l],
              "ln_v_g": ln_v_g[l], "ln_v_b": ln_v_b[l], "w_s": w_s[l], "b_s": b_s[l], "g_ffn2": g_ffn2[l]}
        for name, w in (("w1_ffn1", w1_ffn1), ("w3_ffn1", w3_ffn1), ("w2_ffn1", w2_ffn1), ("w_in", w_in),
                        ("w_pa", w_pa), ("w_pb", w_pb), ("w_o", w_o),
                        ("w1_ffn2", w1_ffn2), ("w3_ffn2", w3_ffn2), ("w2_ffn2", w2_ffn2)):
            lw[name] = w[l].astype(BF16)
        mod = _adaln(c_all, w_ada[l], b_ada[l]).reshape(-1, N_SUB, 3, 1, D_MODEL)
        mods_p = [[mod[:nb_p, i, t] for t in range(3)] for i in range(N_SUB)]
        mods_s = [[mod[nb_p:nb_p + nb_s, i, t] for t in range(3)] for i in range(N_SUB)]
        sk = sinks[l]

        def attend_p(q, k, v):
            return _attn_prompt(q, k, v, sk, tq=512), None

        def attend_s(q, k, v, l=l):
            o, k_win, v_win = _attn_sample(q, k, v, cache_swa_k[l].reshape(nb_s, win, D_KV),
                                           cache_swa_v[l].reshape(nb_s, win, D_KV), sk)
            return o, (k_win, v_win)

        y_p, k_p, v_p, _, _ = _layer(y_p, mods_p, lw, consts, rope_p, attend_p,
                                     bb=1, ts=512, tf=512, tn=512, chunk=GMLP_CHUNK, emit_vn_f32=False)
        y_s, _, _, (k_s, v_s), (vn_s,) = _layer(y_s, mods_s, lw, consts, rope_s, attend_s,
                                                bb=nb_s, ts=dec_seq, tf=512, tn=512,
                                                chunk=min(dec_seq, GMLP_CHUNK), emit_vn_f32=True)
        kp_l.append(k_p[:, -win:].reshape(nb_p, win, N_KV_HEADS, HEAD_DIM))
        vp_l.append(v_p[:, -win:].reshape(nb_p, win, N_KV_HEADS, HEAD_DIM))
        ks_l.append(k_s.reshape(nb_s, win, N_KV_HEADS, HEAD_DIM))
        vs_l.append(v_s.reshape(nb_s, win, N_KV_HEADS, HEAD_DIM))
        gv_l.append(vn_s)
    return (y_p, y_s, jnp.stack(kp_l), jnp.stack(vp_l), jnp.stack(ks_l), jnp.stack(vs_l), jnp.stack(gv_l))
```
